```python
import jax, jax.numpy as jnp
from jax import lax
import numpy as np

D_MODEL = 1024
BATCH = 8
SEQ = 4096
DEPTH = 2
DEC_BATCH = 32
DEC_SEQ = 32
PAST_LEN = 2048

CHUNK = 64
N_META = 16
Q_BLOCK = 128
N_MIXERS = 2
N_FOX = (DEPTH + 1) // 2
N_MLSTM = DEPTH // 2
FOX_HEADS = 16
FOX_HEAD_DIM = D_MODEL // FOX_HEADS
MLSTM_HEADS = 4
MLSTM_DV = D_MODEL // MLSTM_HEADS
MLSTM_DK = MLSTM_DV // 2
D_FF = 4 * D_MODEL
EPS = 1e-6
NEG = -1e30
FOX_IN = 3 * D_MODEL + FOX_HEADS
FOX_SPLITS = [D_MODEL, 2 * D_MODEL, 3 * D_MODEL]
_HK = MLSTM_HEADS * MLSTM_DK
_HV = MLSTM_HEADS * MLSTM_DV
MLSTM_IN = 2 * _HK + 2 * _HV + 2 * MLSTM_HEADS
MLSTM_SPLITS = [_HK, 2 * _HK, 2 * _HK + _HV, 2 * _HK + 2 * _HV, 2 * _HK + 2 * _HV + MLSTM_HEADS]
MLSTM_PAD = (-N_META) % CHUNK

kernel_name = "fox_mlstm_streaming_step"


def rmsnorm(x, g):
    xf = x.astype(jnp.float32)
    y = xf * lax.rsqrt(jnp.mean(xf * xf, axis=-1, keepdims=True) + EPS)
    return (y * g.astype(jnp.float32)).astype(x.dtype)


def sq_relu_mlp(h, w_up, w_down):
    u = jnp.einsum('btd,df->btf', h, w_up)
    return jnp.einsum('btf,fd->btd', jnp.square(jax.nn.relu(u)), w_down)


def fox_attend(q, k, v, Fq, Fk, q_offset):
    B, Tq, H, dh = q.shape
    Tk = k.shape[1]
    blk = min(Q_BLOCK, Tq)
    nb = -(-Tq // blk)
    pad = nb * blk - Tq
    qp = jnp.pad(q, ((0, 0), (0, pad), (0, 0), (0, 0)))
    Fqp = jnp.pad(Fq, ((0, 0), (0, pad), (0, 0)))
    q_blocks = qp.reshape(B, nb, blk, H, dh).transpose(1, 0, 2, 3, 4)
    F_blocks = Fqp.reshape(B, nb, blk, H).transpose(1, 0, 3, 2)
    starts = q_offset + jnp.arange(nb) * blk
    Fk_t = Fk.transpose(0, 2, 1)
    kpos = jnp.arange(Tk)
    scale = FOX_HEAD_DIM ** -0.5

    def one_block(args):
        qb, Fb, s0 = args
        qpos = s0 + jnp.arange(blk)
        logits = jnp.einsum('bqhd,bkhd->bhqk', qb, k, preferred_element_type=jnp.float32) * scale
        logits = logits + Fb[..., :, None] - Fk_t[..., None, :]
        logits = jnp.where(kpos[None, None, None, :] <= qpos[None, None, :, None], logits, -jnp.inf)
        p = jax.nn.softmax(logits, axis=-1)
        return jnp.einsum('bhqk,bkhd->bqhd', p.astype(v.dtype), v)

    out = lax.map(one_block, (q_blocks, F_blocks, starts))
    return out.transpose(1, 0, 2, 3, 4).reshape(B, nb * blk, H, dh)[:, :Tq]


def fox_mixer(h, w_in, b_f, g_q, g_k, w_out, past):
    B, T, _ = h.shape
    proj = jnp.einsum('btd,de->bte', h, w_in)
    q, k, v, f = jnp.split(proj, FOX_SPLITS, axis=-1)
    shp = (B, T, FOX_HEADS, FOX_HEAD_DIM)
    q = rmsnorm(q.reshape(shp), g_q)
    k = rmsnorm(k.reshape(shp), g_k)
    v = v.reshape(shp)
    logf = jax.nn.log_sigmoid((f + b_f).astype(jnp.float32))
    if past is None:
        k_all, v_all, logf_all, offset = k, v, logf, 0
    else:
        k_past, v_past, logf_past = past
        offset = k_past.shape[1]
        k_all = jnp.concatenate([k_past.astype(k.dtype), k], axis=1)
        v_all = jnp.concatenate([v_past.astype(v.dtype), v], axis=1)
        logf_all = jnp.concatenate([logf_past.astype(jnp.float32), logf], axis=1)
    F = jnp.cumsum(logf_all, axis=1)
    o = fox_attend(q, k_all, v_all, F[:, offset:], F, offset)
    y = jnp.einsum('bte,ed->btd', o.reshape(B, T, D_MODEL), w_out)
    return y, (k, v, logf)


def mlstm_chunkwise(q, k, v, i_pre, logf, C0, n0, m0, block):
    B, L, H, _ = q.shape
    nb = L // block

    def to_blocks(a):
        a = a.reshape((B, nb, block) + a.shape[2:])
        return jnp.moveaxis(a, (1, 3), (0, 2))

    tri = jnp.tril(jnp.ones((block, block), dtype=bool))

    def step(carry, xs):
        C, n, m = carry
        qb, kb, vb, ib, fb = xs
        b = jnp.cumsum(fb, axis=-1)
        Dm = jnp.where(tri, b[..., :, None] - b[..., None, :] + ib[..., None, :], -jnp.inf)
        inter = b + m[..., None]
        mt = jnp.maximum(inter, jnp.max(Dm, axis=-1))
        w_inter = jnp.exp(inter - mt)
        S = jnp.einsum('bhtk,bhsk->bhts', qb, kb) * jnp.exp(Dm - mt[..., None])
        num = w_inter[..., None] * jnp.einsum('bhtk,bhvk->bhtv', qb, C) + jnp.einsum('bhts,bhsv->bhtv', S, vb)
        den = w_inter * jnp.einsum('bhtk,bhk->bht', qb, n) + jnp.sum(S, axis=-1)
        h = num / jnp.maximum(jnp.abs(den), jnp.exp(-mt))[..., None]
        m_new = mt[..., -1]
        g = b[..., -1:] - b + ib
        decay = jnp.exp(b[..., -1] + m - m_new)
        wg = jnp.exp(g - m_new[..., None])
        C_new = decay[..., None, None] * C + jnp.einsum('bhs,bhsv,bhsk->bhvk', wg, vb, kb)
        n_new = decay[..., None] * n + jnp.einsum('bhs,bhsk->bhk', wg, kb)
        return (C_new, n_new, m_new), h

    xs = (to_blocks(q), to_blocks(k), to_blocks(v), to_blocks(i_pre), to_blocks(logf))
    (C, n, m), hs = lax.scan(step, (C0, n0, m0), xs)
    h = jnp.moveaxis(hs, (0, 2), (1, 3)).reshape(B, L, H, hs.shape[-1])
    return h, C, n, m


def mlstm_mixer(h, w_in, b_i, b_f, g_h, w_out, C0, n0, m0, block, pad_front):
    B, T, _ = h.shape
    H, DK, DV = MLSTM_HEADS, MLSTM_DK, MLSTM_DV
    proj = jnp.einsum('btd,de->bte', h, w_in).astype(jnp.float32)
    q, k, v, o, ig, fg = jnp.split(proj, MLSTM_SPLITS, axis=-1)
    q = q.reshape(B, T, H, DK)
    k = k.reshape(B, T, H, DK) * (DK ** -0.5)
    v = v.reshape(B, T, H, DV)
    i_pre = ig + b_i.astype(jnp.float32)
    logf = jax.nn.log_sigmoid(fg + b_f.astype(jnp.float32))
    if pad_front:
        p4 = ((0, 0), (pad_front, 0), (0, 0), (0, 0))
        p3 = ((0, 0), (pad_front, 0), (0, 0))
        q, k, v = jnp.pad(q, p4), jnp.pad(k, p4), jnp.pad(v, p4)
        i_pre = jnp.pad(i_pre, p3, constant_values=NEG)
        logf = jnp.pad(logf, p3)
    hh, C, n, m = mlstm_chunkwise(q, k, v, i_pre, logf, C0.astype(jnp.float32),
                                  n0.astype(jnp.float32), m0.astype(jnp.float32), block)
    hh = rmsnorm(hh[:, pad_front:], g_h)
    o = jax.nn.sigmoid(o.reshape(B, T, H, DV))
    y = jnp.einsum('bte,ed->btd', (hh * o).reshape(B, T, H * DV).astype(h.dtype), w_out)
    return y, (C, n, m)


def run_trunk(x, fox_past, mlstm_init, mlstm_block, mlstm_pad,
              g_mix, g_ffn, fox_w_in, fox_b_f, fox_g_q, fox_g_k, fox_w_out,
              mlstm_w_in, mlstm_b_i, mlstm_b_f, mlstm_g_h, mlstm_w_out,
              ffn_w_up, ffn_w_down, g_final):
    fox_new, mlstm_new = [], []
    for i in range(DEPTH):
        h = rmsnorm(x, g_mix[i])
        j = i // N_MIXERS
        if i % N_MIXERS == 0:
            past = None if fox_past is None else (fox_past[0][j], fox_past[1][j], fox_past[2][j])
            y, st = fox_mixer(h, fox_w_in[j], fox_b_f[j], fox_g_q[j], fox_g_k[j], fox_w_out[j], past)
            fox_new.append(st)
        else:
            y, st = mlstm_mixer(h, mlstm_w_in[j], mlstm_b_i[j], mlstm_b_f[j], mlstm_g_h[j], mlstm_w_out[j],
                                mlstm_init[0][j], mlstm_init[1][j], mlstm_init[2][j], mlstm_block, mlstm_pad)
            mlstm_new.append(st)
        x = x + y
        x = x + sq_relu_mlp(rmsnorm(x, g_ffn[i]), ffn_w_up[i], ffn_w_down[i])
    out = rmsnorm(x, g_final)
    fk = jnp.stack([s[0] for s in fox_new])
    fv = jnp.stack([s[1] for s in fox_new])
    fl = jnp.stack([s[2] for s in fox_new])
    mC = jnp.stack([s[0] for s in mlstm_new])
    mn = jnp.stack([s[1] for s in mlstm_new])
    mm = jnp.stack([s[2] for s in mlstm_new])
    return out, fk, fv, fl, mC, mn, mm


def setup_inputs(seed: int = 0) -> dict:
    key = jax.random.key(seed)
    ks = jax.random.split(key, 26)
    f32 = jnp.float32

    def nrm(k, shape, scale):
        return jax.random.normal(k, shape, f32) * scale

    H, dh = FOX_HEADS, FOX_HEAD_DIM
    MH, DK, DV = MLSTM_HEADS, MLSTM_DK, MLSTM_DV
    return {
        'x_prompt': nrm(ks[0], (BATCH, SEQ, D_MODEL), 1.0),
        'x_sample': nrm(ks[1], (DEC_BATCH, DEC_SEQ, D_MODEL), 1.0),
        'cache_fox_k': nrm(ks[2], (N_FOX, DEC_BATCH, PAST_LEN, H, dh), 1.0),
        'cache_fox_v': nrm(ks[3], (N_FOX, DEC_BATCH, PAST_LEN, H, dh), 1.0),
        'cache_fox_logf': jax.nn.log_sigmoid(3.0 + nrm(ks[4], (N_FOX, DEC_BATCH, PAST_LEN, H), 1.0)),
        'state_mlstm_C': nrm(ks[5], (N_MLSTM, DEC_BATCH, MH, DV, DK), 0.1),
        'state_mlstm_n': nrm(ks[6], (N_MLSTM, DEC_BATCH, MH, DK), 0.1),
        'state_mlstm_m': nrm(ks[7], (N_MLSTM, DEC_BATCH, MH), 1.0),
        'meta_tokens': nrm(ks[8], (N_META, D_MODEL), 1.0),
        'g_mix': 1.0 + nrm(ks[9], (DEPTH, D_MODEL), 0.02),
        'g_ffn': 1.0 + nrm(ks[10], (DEPTH, D_MODEL), 0.02),
        'fox_w_in': nrm(ks[11], (N_FOX, D_MODEL, FOX_IN), D_MODEL ** -0.5),
        'fox_b_f': 3.0 + nrm(ks[12], (N_FOX, H), 0.1),
        'fox_g_q': 1.0 + nrm(ks[13], (N_FOX, dh), 0.02),
        'fox_g_k': 1.0 + nrm(ks[14], (N_FOX, dh), 0.02),
        'fox_w_out': nrm(ks[15], (N_FOX, D_MODEL, D_MODEL), D_MODEL ** -0.5),
        'mlstm_w_in': nrm(ks[16], (N_MLSTM, D_MODEL, MLSTM_IN), D_MODEL ** -0.5),
        'mlstm_b_i': nrm(ks[17], (N_MLSTM, MH), 0.1),
        'mlstm_b_f': 3.0 + nrm(ks[18], (N_MLSTM, MH), 0.1),
        'mlstm_g_h': 1.0 + nrm(ks[19], (N_MLSTM, MH, DV), 0.02),
        'mlstm_w_out': nrm(ks[20], (N_MLSTM, MH * DV, D_MODEL), (MH * DV) ** -0.5),
        'ffn_w_up': nrm(ks[21], (DEPTH, D_MODEL, D_FF), D_MODEL ** -0.5),
        'ffn_w_down': nrm(ks[22], (DEPTH, D_FF, D_MODEL), D_FF ** -0.5),
        'g_final': 1.0 + nrm(ks[23], (D_MODEL,), 0.02),
    }


def reference(x_prompt, x_sample, cache_fox_k, cache_fox_v, cache_fox_logf,
              state_mlstm_C, state_mlstm_n, state_mlstm_m, meta_tokens,
              g_mix, g_ffn, fox_w_in, fox_b_f, fox_g_q, fox_g_k, fox_w_out,
              mlstm_w_in, mlstm_b_i, mlstm_b_f, mlstm_g_h, mlstm_w_out,
              ffn_w_up, ffn_w_down, g_final):
    weights = (g_mix, g_ffn, fox_w_in, fox_b_f, fox_g_q, fox_g_k, fox_w_out,
               mlstm_w_in, mlstm_b_i, mlstm_b_f, mlstm_g_h, mlstm_w_out,
               ffn_w_up, ffn_w_down, g_final)

    B = x_prompt.shape[0]
    meta = jnp.broadcast_to(meta_tokens.astype(x_prompt.dtype)[None], (B, N_META, D_MODEL))
    xp = jnp.concatenate([meta, x_prompt], axis=1)
    init = (jnp.zeros((N_MLSTM, B, MLSTM_HEADS, MLSTM_DV, MLSTM_DK), jnp.float32),
            jnp.zeros((N_MLSTM, B, MLSTM_HEADS, MLSTM_DK), jnp.float32),
            jnp.zeros((N_MLSTM, B, MLSTM_HEADS), jnp.float32))
    yp, fk_p, fv_p, fl_p, mC_p, mn_p, mm_p = run_trunk(xp, None, init, CHUNK, MLSTM_PAD, *weights)
    y_prompt = yp[:, N_META:]

    T = x_sample.shape[1]
    ys, fk_s, fv_s, fl_s, mC_s, mn_s, mm_s = run_trunk(
        x_sample, (cache_fox_k, cache_fox_v, cache_fox_logf),
        (state_mlstm_C, state_mlstm_n, state_mlstm_m), T, 0, *weights)

    return (y_prompt, ys, fk_p, fv_p, fl_p, mC_p, mn_p, mm_p, fk_s, fv_s, fl_s, mC_s, mn_s, mm_s)
```

```python
import functools

import jax
import jax.numpy as jnp
from jax import lax
from jax.experimental import pallas as pl
from jax.experimental.pallas import tpu as pltpu

F32 = jnp.float32
BF16 = jnp.bfloat16

D_MODEL = 1024
N_META = 16
FOX_HEADS = 16
FOX_HEAD_DIM = 64
MLSTM_HEADS = 4
MLSTM_DK = 128
MLSTM_DV = 256
D_FF = 4096
EPS = 1e-6
NEG = -1e30
BIG = 1e30

LANES = 128
ROW_TILE = 512
PAD_T = 128
MLSTM_CHUNK = 128
VMEM_LIMIT = 56 * 1024 * 1024


def _cparams(sem):
    return pltpu.CompilerParams(dimension_semantics=sem, vmem_limit_bytes=VMEM_LIMIT)


def _const_spec(shape):
    nd = len(shape)
    return pl.BlockSpec(shape, lambda *_: (0,) * nd, pipeline_mode=pl.Buffered(1))


def _log_sigmoid(f):
    return jnp.minimum(f, 0.0) - jnp.log1p(jnp.exp(-jnp.abs(f)))


def _split3(x):
    hi = x.astype(BF16)
    r1 = x - hi.astype(F32)
    mid = r1.astype(BF16)
    lo = (r1 - mid.astype(F32)).astype(BF16)
    return hi, mid, lo


def _rms_rows(x, g):
    ms = jnp.mean(x * x, axis=-1, keepdims=True)
    return x * lax.rsqrt(ms + EPS) * g


def _fox_inproj_kernel(x_ref, g_ref, wqkv_ref, wf_ref, bf_ref, gq_ref, gk_ref, s_ref, st2_ref,
                       q_ref, k32_ref, v32_ref, kbf_ref, vT_ref, logf_ref):
    h = _rms_rows(x_ref[...], g_ref[...]).astype(BF16)
    d = D_MODEL
    q = jnp.dot(h, wqkv_ref[:, 0:d], preferred_element_type=F32)
    k = jnp.dot(h, wqkv_ref[:, d:2 * d], preferred_element_type=F32)
    v = jnp.dot(h, wqkv_ref[:, 2 * d:3 * d], preferred_element_type=F32)
    f = jnp.dot(h, wf_ref[...], preferred_element_type=F32) + bf_ref[...]

    def head_norm(y, gain):
        ss = jnp.dot((y * y).astype(BF16), s_ref[...], preferred_element_type=F32)
        r = lax.rsqrt(ss * (1.0 / FOX_HEAD_DIM) + EPS)
        r_hi = r.astype(BF16)
        r_lo = (r - r_hi.astype(F32)).astype(BF16)
        rb = jnp.dot(jnp.concatenate([r_hi, r_lo], axis=1), st2_ref[...], preferred_element_type=F32)
        return y * rb * gain

    qn = head_norm(q, gq_ref[...])
    kn = head_norm(k, gk_ref[...])
    q_ref[...] = qn.astype(BF16)
    k32_ref[...] = kn
    v32_ref[...] = v
    kbf_ref[...] = kn.astype(BF16)
    vT_ref[0] = v.T.astype(BF16)
    logf_ref[...] = _log_sigmoid(f)


def fox_inproj(x, g, wqkv, wf, bf, gq, gk, s_mat, st2_mat, tm):
    rows = x.shape[0]
    nt = rows // tm
    d = D_MODEL
    row_spec = pl.BlockSpec((tm, d), lambda i: (i, 0))
    out_shape = (
        jax.ShapeDtypeStruct((rows, d), BF16),
        jax.ShapeDtypeStruct((rows, d), F32),
        jax.ShapeDtypeStruct((rows, d), F32),
        jax.ShapeDtypeStruct((rows, d), BF16),
        jax.ShapeDtypeStruct((nt, d, tm), BF16),
        jax.ShapeDtypeStruct((rows, LANES), F32),
    )
    return pl.pallas_call(
        _fox_inproj_kernel,
        out_shape=out_shape,
        grid=(nt,),
        in_specs=[row_spec, _const_spec((1, d)), _const_spec((d, 3 * d)), _const_spec((d, LANES)),
                  _const_spec((1, LANES)), _const_spec((1, d)), _const_spec((1, d)),
                  _const_spec((d, LANES)), _const_spec((2 * LANES, d))],
        out_specs=(row_spec, row_spec, row_spec, row_spec,
                   pl.BlockSpec((1, d, tm), lambda i: (i, 0, 0)),
                   pl.BlockSpec((tm, LANES), lambda i: (i, 0))),
        compiler_params=_cparams(("parallel",)),
        name="fox_inproj",
    )(x, g, wqkv, wf, bf, gq, gk, s_mat, st2_mat)


def _cumsum_aug_kernel(x_ref, m_ref, ka_ref, qa_ref, carry_ref):
    @pl.when(pl.program_id(1) == 0)
    def _():
        carry_ref[...] = jnp.zeros_like(carry_ref)

    x = x_ref[...]
    tc = x.shape[0]
    hi, mid, lo = _split3(x)
    r = jnp.dot(m_ref[...], jnp.concatenate([hi, mid, lo], axis=1), preferred_element_type=F32)
    fcum = r[:, 0:LANES] + r[:, LANES:2 * LANES] + r[:, 2 * LANES:3 * LANES] + carry_ref[...]
    carry_ref[...] = fcum[tc - 1:tc, :]

    lane = lax.broadcasted_iota(jnp.int32, (1, LANES), 1)
    valid = (lane < FOX_HEADS).astype(F32)
    fh, fm, fl = _split3(fcum)
    packed = (fh.astype(F32) * valid
              + pltpu.roll(fm.astype(F32) * valid, FOX_HEADS, 1)
              + pltpu.roll(fl.astype(F32) * valid, 2 * FOX_HEADS, 1))
    ones_hi = ((lane >= 3 * FOX_HEADS) & (lane < 6 * FOX_HEADS)).astype(F32)
    ones_lo = (lane < 3 * FOX_HEADS).astype(F32)
    ka_ref[...] = (packed + ones_hi).astype(BF16)
    qa_ref[...] = (pltpu.roll(packed, 3 * FOX_HEADS, 1) - ones_lo).astype(BF16)


def cumsum_aug(logf, m_mat, batch, tc):
    rows = logf.shape[0]
    nc = rows // batch // tc
    spec = pl.BlockSpec((tc, LANES), lambda b, c: (b * nc + c, 0))
    return pl.pallas_call(
        _cumsum_aug_kernel,
        out_shape=(jax.ShapeDtypeStruct((rows, LANES), BF16), jax.ShapeDtypeStruct((rows, LANES), BF16)),
        grid=(batch, nc),
        in_specs=[spec, _const_spec((tc, tc))],
        out_specs=(spec, spec),
        scratch_shapes=[pltpu.VMEM((1, LANES), F32)],
        compiler_params=_cparams(("parallel", "arbitrary")),
        name="cumsum_aug",
    )(logf, m_mat)


def _prep_queries(q_ref, qa_ref, qx_ref, pair):
    lane = lax.broadcasted_iota(jnp.int32, (1, LANES), 1)
    qf = q_ref[...].astype(F32)
    qaf = qa_ref[...].astype(F32)
    for hh in range(2):
        head = 2 * pair + hh
        qmask = ((lane // FOX_HEAD_DIM) == hh).astype(F32)
        amask = (((lane % FOX_HEADS) == head) & (lane < 6 * FOX_HEADS)).astype(F32)
        qx_ref[hh] = jnp.concatenate([(qf * qmask).astype(BF16), (qaf * amask).astype(BF16)], axis=1)


def _init_stats(m_ref, l_ref, acc_ref):
    m_ref[...] = jnp.full_like(m_ref, NEG)
    l_ref[...] = jnp.zeros_like(l_ref)
    acc_ref[...] = jnp.zeros_like(acc_ref)


def _attend_block(kx, vT, qx_ref, m_ref, l_ref, acc_ref, mask):
    for hh in range(2):
        st = lax.dot_general(kx, qx_ref[hh], (((1,), (1,)), ((), ())), preferred_element_type=F32)
        if mask is not None:
            st = jnp.where(mask, st, NEG)
        m_prev = m_ref[hh]
        m_new = jnp.maximum(m_prev, jnp.max(st, axis=0, keepdims=True))
        alpha = jnp.exp(m_prev - m_new)
        p = jnp.exp(st - m_new)
        l_ref[hh] = alpha * l_ref[hh] + jnp.sum(p, axis=0, keepdims=True)
        vh = vT[hh * FOX_HEAD_DIM:(hh + 1) * FOX_HEAD_DIM, :]
        acc_ref[hh] = alpha * acc_ref[hh] + jnp.dot(vh, p.astype(BF16), preferred_element_type=F32)
        m_ref[hh] = m_new


def _finish(o_ref, l_ref, acc_ref):
    ot = jnp.concatenate([acc_ref[0] / l_ref[0], acc_ref[1] / l_ref[1]], axis=0)
    o_ref[...] = ot.T.astype(BF16)


def _causal_mask(n, bq):
    row = lax.broadcasted_iota(jnp.int32, (n, bq), 0)
    col = lax.broadcasted_iota(jnp.int32, (n, bq), 1)
    return row <= col


def _attn_main_kernel(q_ref, qa_ref, k_ref, ka_ref, vT_ref, km_ref, kam_ref, vTm_ref, o_ref,
                      qx_ref, m_ref, l_ref, acc_ref):
    pair = pl.program_id(1)
    i = pl.program_id(2)
    bq = q_ref.shape[0]
    _prep_queries(q_ref, qa_ref, qx_ref, pair)
    _init_stats(m_ref, l_ref, acc_ref)

    meta_rows = lax.broadcasted_iota(jnp.int32, (PAD_T, bq), 0)
    _attend_block(jnp.concatenate([km_ref[...], kam_ref[...]], axis=1), vTm_ref[0],
                  qx_ref, m_ref, l_ref, acc_ref, meta_rows < N_META)

    def body(j, carry):
        off = pl.multiple_of(j * bq, bq)
        kx = jnp.concatenate([k_ref[pl.ds(off, bq), :], ka_ref[pl.ds(off, bq), :]], axis=1)
        _attend_block(kx, vT_ref[j], qx_ref, m_ref, l_ref, acc_ref, None)
        return carry

    lax.fori_loop(0, i, body, 0)

    off = pl.multiple_of(i * bq, bq)
    kx = jnp.concatenate([k_ref[pl.ds(off, bq), :], ka_ref[pl.ds(off, bq), :]], axis=1)
    _attend_block(kx, vT_ref[i], qx_ref, m_ref, l_ref, acc_ref, _causal_mask(bq, bq))
    _finish(o_ref, l_ref, acc_ref)


def attn_main(q, qa, k, ka, vT, km, kam, vTm, batch, seq):
    rows = q.shape[0]
    bq = ROW_TILE
    nq = seq // bq
    hp = FOX_HEADS // 2
    return pl.pallas_call(
        _attn_main_kernel,
        out_shape=jax.ShapeDtypeStruct((rows, D_MODEL), BF16),
        grid=(batch, hp, nq),
        in_specs=[
            pl.BlockSpec((bq, LANES), lambda b, p, i: (b * nq + i, p)),
            pl.BlockSpec((bq, LANES), lambda b, p, i: (b * nq + i, 0)),
            pl.BlockSpec((seq, LANES), lambda b, p, i: (b, p)),
            pl.BlockSpec((seq, LANES), lambda b, p, i: (b, 0)),
            pl.BlockSpec((nq, LANES, bq), lambda b, p, i: (b, p, 0)),
            pl.BlockSpec((PAD_T, LANES), lambda b, p, i: (0, p)),
            pl.BlockSpec((PAD_T, LANES), lambda b, p, i: (0, 0)),
            pl.BlockSpec((1, LANES, PAD_T), lambda b, p, i: (0, p, 0)),
        ],
        out_specs=pl.BlockSpec((bq, LANES), lambda b, p, i: (b * nq + i, p)),
        scratch_shapes=[pltpu.VMEM((2, bq, 2 * LANES), BF16), pltpu.VMEM((2, 1, bq), F32),
                        pltpu.VMEM((2, 1, bq), F32), pltpu.VMEM((2, FOX_HEAD_DIM, bq), F32)],
        compiler_params=_cparams(("parallel", "parallel", "arbitrary")),
        name="attn_main",
    )(q, qa, k, ka, vT, km, kam, vTm)


def _attn_small_kernel(*refs, n_pre, pre_blk):
    if n_pre:
        (q_ref, qa_ref, k_ref, ka_ref, vT_ref, kc_ref, vc_ref, kap_ref, o_ref,
         qx_ref, m_ref, l_ref, acc_ref) = refs
    else:
        q_ref, qa_ref, k_ref, ka_ref, vT_ref, o_ref, qx_ref, m_ref, l_ref, acc_ref = refs
    pair = pl.program_id(1)
    bq = q_ref.shape[0]
    _prep_queries(q_ref, qa_ref, qx_ref, pair)
    _init_stats(m_ref, l_ref, acc_ref)
    for c in range(n_pre):
        sl = slice(c * pre_blk, (c + 1) * pre_blk)
        kx = jnp.concatenate([kc_ref[sl, :].astype(BF16), kap_ref[sl, :]], axis=1)
        _attend_block(kx, vc_ref[sl, :].T.astype(BF16), qx_ref, m_ref, l_ref, acc_ref, None)
    kx = jnp.concatenate([k_ref[...], ka_ref[...]], axis=1)
    _attend_block(kx, vT_ref[0], qx_ref, m_ref, l_ref, acc_ref, _causal_mask(bq, bq))
    _finish(o_ref, l_ref, acc_ref)


def attn_small(q, qa, k, ka, vT, batch, cache=None):
    rows = q.shape[0]
    bq = PAD_T
    hp = FOX_HEADS // 2
    in_specs = [
        pl.BlockSpec((bq, LANES), lambda b, p: (b, p)),
        pl.BlockSpec((bq, LANES), lambda b, p: (b, 0)),
        pl.BlockSpec((bq, LANES), lambda b, p: (b, p)),
        pl.BlockSpec((bq, LANES), lambda b, p: (b, 0)),
        pl.BlockSpec((1, LANES, bq), lambda b, p: (b, p, 0)),
    ]
    args = [q, qa, k, ka, vT]
    n_pre = 0
    if cache is not None:
        kc, vc, kap = cache
        past = kc.shape[0] // batch
        n_pre = past // ROW_TILE
        in_specs += [pl.BlockSpec((past, LANES), lambda b, p: (b, p)),
                     pl.BlockSpec((past, LANES), lambda b, p: (b, p)),
                     pl.BlockSpec((past, LANES), lambda b, p: (b, 0))]
        args += [kc, vc, kap]
    return pl.pallas_call(
        functools.partial(_attn_small_kernel, n_pre=n_pre, pre_blk=ROW_TILE),
        out_shape=jax.ShapeDtypeStruct((rows, D_MODEL), BF16),
        grid=(batch, hp),
        in_specs=in_specs,
        out_specs=pl.BlockSpec((bq, LANES), lambda b, p: (b, p)),
        scratch_shapes=[pltpu.VMEM((2, bq, 2 * LANES), BF16), pltpu.VMEM((2, 1, bq), F32),
                        pltpu.VMEM((2, 1, bq), F32), pltpu.VMEM((2, FOX_HEAD_DIM, bq), F32)],
        compiler_params=_cparams(("parallel", "parallel")),
        name="attn_small",
    )(*args)


def _post_kernel(a_ref, x_ref, wo_ref, g_ref, wu_ref, wd_ref, gf_ref, o_ref, *, final_norm):
    x1 = x_ref[...] + jnp.dot(a_ref[...], wo_ref[...], preferred_element_type=F32)
    hn = _rms_rows(x1, g_ref[...]).astype(BF16)
    acc = x1
    d = D_MODEL
    for c in range(D_FF // d):
        u = jnp.dot(hn, wu_ref[:, c * d:(c + 1) * d], preferred_element_type=F32)
        r = jnp.maximum(u, 0.0)
        acc = acc + jnp.dot((r * r).astype(BF16), wd_ref[c * d:(c + 1) * d, :], preferred_element_type=F32)
    if final_norm:
        acc = _rms_rows(acc, gf_ref[...])
    o_ref[...] = acc


def post(a, x, wo, g, wu, wd, gf, tm, final_norm):
    rows = x.shape[0]
    d = D_MODEL
    row_spec = pl.BlockSpec((tm, d), lambda i: (i, 0))
    return pl.pallas_call(
        functools.partial(_post_kernel, final_norm=final_norm),
        out_shape=jax.ShapeDtypeStruct((rows, d), F32),
        grid=(rows // tm,),
        in_specs=[row_spec, row_spec, _const_spec((d, d)), _const_spec((1, d)),
                  _const_spec((d, D_FF)), _const_spec((D_FF, d)), _const_spec((1, d))],
        out_specs=row_spec,
        compiler_params=_cparams(("parallel",)),
        name="post",
    )(a, x, wo, g, wu, wd, gf)


def _mlstm_inproj_kernel(x_ref, g_ref, w_ref, wg_ref, q_ref, k_ref, v_ref, o_ref, gate_ref):
    h = _rms_rows(x_ref[...], g_ref[...]).astype(BF16)
    hk = MLSTM_HEADS * MLSTM_DK
    hv = MLSTM_HEADS * MLSTM_DV
    q_ref[...] = jnp.dot(h, w_ref[:, 0:hk], preferred_element_type=F32).astype(BF16)
    k = jnp.dot(h, w_ref[:, hk:2 * hk], preferred_element_type=F32)
    k_ref[...] = (k * (MLSTM_DK ** -0.5)).astype(BF16)
    v_ref[...] = jnp.dot(h, w_ref[:, 2 * hk:2 * hk + hv], preferred_element_type=F32).astype(BF16)
    o_ref[...] = jnp.dot(h, w_ref[:, 2 * hk + hv:2 * hk + 2 * hv], preferred_element_type=F32)
    gate_ref[...] = jnp.dot(h, wg_ref[...], preferred_element_type=F32)


def mlstm_inproj(x, g, w, wg, tm):
    rows = x.shape[0]
    d = D_MODEL
    hk = MLSTM_HEADS * MLSTM_DK
    hv = MLSTM_HEADS * MLSTM_DV

    def rs(width):
        return pl.BlockSpec((tm, width), lambda i: (i, 0))

    return pl.pallas_call(
        _mlstm_inproj_kernel,
        out_shape=(jax.ShapeDtypeStruct((rows, hk), BF16), jax.ShapeDtypeStruct((rows, hk), BF16),
                   jax.ShapeDtypeStruct((rows, hv), BF16), jax.ShapeDtypeStruct((rows, hv), F32),
                   jax.ShapeDtypeStruct((rows, LANES), F32)),
        grid=(rows // tm,),
        in_specs=[rs(d), _const_spec((1, d)), _const_spec((d, 2 * hk + 2 * hv)), _const_spec((d, LANES))],
        out_specs=(rs(hk), rs(hk), rs(hv), rs(hv), rs(LANES)),
        compiler_params=_cparams(("parallel",)),
        name="mlstm_inproj",
    )(x, g, w, wg)


def _mlstm_kernel(q_ref, k_ref, v_ref, o_ref, gc_ref, gr_ref, brow_ref, bcol_ref, gh_ref,
                  tril_ref, triu_ref, c0_ref, n0_ref, m0_ref,
                  a_ref, cout_ref, nout_ref, mout_ref, ct_sc, n_sc, m_sc):
    c = pl.program_id(1)
    nh, dk, dv = MLSTM_HEADS, MLSTM_DK, MLSTM_DV
    L = q_ref.shape[0]

    @pl.when(c == 0)
    def _():
        for h in range(nh):
            ct_sc[h] = c0_ref[0, h].T
            n_sc[h] = n0_ref[0, h]
            m_sc[h] = m0_ref[0, h]

    gc = gc_ref[...] + brow_ref[...]
    flc = _log_sigmoid(gc)
    hi, mid, lo = _split3(flc)
    r = jnp.dot(tril_ref[...], jnp.concatenate([hi, mid, lo], axis=1), preferred_element_type=F32)
    bc = r[:, 0:LANES] + r[:, LANES:2 * LANES] + r[:, 2 * LANES:3 * LANES]

    gr = gr_ref[...] + bcol_ref[...]
    flr = _log_sigmoid(gr)
    hi, mid, lo = _split3(flr)
    tu = triu_ref[...]
    br = (jnp.dot(hi, tu, preferred_element_type=F32) + jnp.dot(mid, tu, preferred_element_type=F32)
          + jnp.dot(lo, tu, preferred_element_type=F32))

    row = lax.broadcasted_iota(jnp.int32, (L, L), 0)
    col = lax.broadcasted_iota(jnp.int32, (L, L), 1)
    tri = col <= row

    for h in range(nh):
        qh = q_ref[:, h * dk:(h + 1) * dk]
        kh = k_ref[:, h * dk:(h + 1) * dk]
        vh = v_ref[:, h * dv:(h + 1) * dv]
        b_col = bc[:, nh + h:nh + h + 1]
        i_col = gc[:, h:h + 1]
        b_row = br[nh + h:nh + h + 1, :]
        i_row = gr[h:h + 1, :]
        m_prev = m_sc[h][:, 0:1]
        ct = ct_sc[h]
        nrow = n_sc[h]

        dm = jnp.where(tri, b_col - b_row + i_row, NEG)
        inter = b_col + m_prev
        mt = jnp.maximum(inter, jnp.max(dm, axis=1, keepdims=True))
        w_inter = jnp.exp(inter - mt)
        qk = lax.dot_general(qh, kh, (((1,), (1,)), ((), ())), preferred_element_type=F32)
        s = qk * jnp.exp(dm - mt)
        num = (w_inter * jnp.dot(qh, ct.astype(BF16), preferred_element_type=F32)
               + jnp.dot(s.astype(BF16), vh, preferred_element_type=F32))
        den = (w_inter * jnp.sum(qh.astype(F32) * nrow, axis=1, keepdims=True)
               + jnp.sum(s, axis=1, keepdims=True))
        hv = num * (1.0 / jnp.maximum(jnp.abs(den), jnp.exp(-mt)))

        m_new = mt[L - 1:L, :]
        b_last = b_col[L - 1:L, :]
        decay = jnp.exp(b_last + m_prev - m_new)
        wg_col = jnp.exp(b_last - b_col + i_col - m_new)
        wg_row = jnp.exp(b_last - b_row + i_row - m_new)
        kt = kh.astype(F32).T.astype(BF16)
        wv = (wg_col * vh.astype(F32)).astype(BF16)
        ct_sc[h] = decay * ct + jnp.dot(kt, wv, preferred_element_type=F32)
        nk = jnp.dot(jnp.broadcast_to(wg_row, (8, L)).astype(BF16), kh, preferred_element_type=F32)
        n_sc[h] = decay * nrow + nk[0:1, :]
        m_sc[h] = jnp.broadcast_to(m_new, (1, LANES))

        hn = _rms_rows(hv, gh_ref[:, h * dv:(h + 1) * dv])
        og = o_ref[:, h * dv:(h + 1) * dv]
        a_ref[:, h * dv:(h + 1) * dv] = (hn * (1.0 / (1.0 + jnp.exp(-og)))).astype(BF16)

    @pl.when(c == pl.num_programs(1) - 1)
    def _():
        for h in range(nh):
            cout_ref[0, h] = ct_sc[h].T
            nout_ref[0, h] = n_sc[h]
            mout_ref[0, h] = m_sc[h]


def mlstm(q, k, v, o, gates_c, gates_r, brow, bcol, gh, c0, n0, m0, batch):
    rows = q.shape[0]
    L = MLSTM_CHUNK
    nc = rows // batch // L
    nh, dk, dv = MLSTM_HEADS, MLSTM_DK, MLSTM_DV
    idx = jnp.arange(L)
    tril = (idx[None, :] <= idx[:, None]).astype(BF16)
    triu = (idx[:, None] <= idx[None, :]).astype(BF16)

    def rs(width):
        return pl.BlockSpec((L, width), lambda b, c: (b * nc + c, 0))

    st4 = lambda *tail: pl.BlockSpec((1, nh) + tail, lambda b, c: (b, 0) + (0,) * len(tail))
    return pl.pallas_call(
        _mlstm_kernel,
        out_shape=(jax.ShapeDtypeStruct((rows, nh * dv), BF16),
                   jax.ShapeDtypeStruct((batch, nh, dv, dk), F32),
                   jax.ShapeDtypeStruct((batch, nh, 1, dk), F32),
                   jax.ShapeDtypeStruct((batch, nh, 1, LANES), F32)),
        grid=(batch, nc),
        in_specs=[rs(nh * dk), rs(nh * dk), rs(nh * dv), rs(nh * dv), rs(LANES),
                  pl.BlockSpec((8, L), lambda b, c: (0, b * nc + c)),
                  _const_spec((1, LANES)), _const_spec((8, 1)), _const_spec((1, nh * dv)),
                  _const_spec((L, L)), _const_spec((L, L)),
                  st4(dv, dk), st4(1, dk), st4(1, LANES)],
        out_specs=(rs(nh * dv), st4(dv, dk), st4(1, dk), st4(1, LANES)),
        scratch_shapes=[pltpu.VMEM((nh, dk, dv), F32), pltpu.VMEM((nh, 1, dk), F32),
                        pltpu.VMEM((nh, 1, LANES), F32)],
        compiler_params=_cparams(("parallel", "arbitrary")),
        name="mlstm",
    )(q, k, v, o, gates_c, gates_r, brow, bcol, gh, tril, triu, c0, n0, m0)


def _pad_rows(a, batch, t, tp, value=0.0):
    w = a.shape[-1]
    a = a.reshape(batch, t, w)
    a = jnp.pad(a, ((0, 0), (0, tp - t), (0, 0)), constant_values=value)
    return a.reshape(batch * tp, w)


def _unpad_rows(a, batch, t, tp):
    w = a.shape[-1]
    return a.reshape(batch, tp, w)[:, :t].reshape(batch * t, w)


def kernel(x_prompt, x_sample, cache_fox_k, cache_fox_v, cache_fox_logf, state_mlstm_C, state_mlstm_n,
           state_mlstm_m, meta_tokens, g_mix, g_ffn, fox_w_in, fox_b_f, fox_g_q, fox_g_k, fox_w_out,
           mlstm_w_in, mlstm_b_i, mlstm_b_f, mlstm_g_h, mlstm_w_out, ffn_w_up, ffn_w_down, g_final):
    d = D_MODEL
    nhf, dh = FOX_HEADS, FOX_HEAD_DIM
    nh, dk, dv = MLSTM_HEADS, MLSTM_DK, MLSTM_DV
    batch, seq, _ = x_prompt.shape
    dbatch, dseq, _ = x_sample.shape
    past = cache_fox_k.shape[2]

    w_in0 = fox_w_in[0]
    wqkv = w_in0[:, :3 * d].astype(BF16)
    wf = jnp.pad(w_in0[:, 3 * d:], ((0, 0), (0, LANES - nhf))).astype(BF16)
    bf = jnp.pad(fox_b_f[0], (0, LANES - nhf)).reshape(1, LANES)
    gq = (jnp.tile(fox_g_q[0], nhf) * (dh ** -0.5)).reshape(1, d)
    gk = jnp.tile(fox_g_k[0], nhf).reshape(1, d)
    head_of = jnp.arange(d) // dh
    s_mat = (head_of[:, None] == jnp.arange(LANES)[None, :]).astype(BF16)
    st2_mat = jnp.concatenate([s_mat.T, s_mat.T], axis=0)
    wo0 = fox_w_out[0].astype(BF16)
    wu = ffn_w_up.astype(BF16)
    wd = ffn_w_down.astype(BF16)
    g_mix2 = g_mix.reshape(-1, 1, d)
    g_ffn2 = g_ffn.reshape(-1, 1, d)
    gfin = g_final.reshape(1, d)
    w_in1 = mlstm_w_in[0]
    wm = w_in1[:, :2 * nh * dk + 2 * nh * dv].astype(BF16)
    wg = jnp.pad(w_in1[:, 2 * nh * dk + 2 * nh * dv:], ((0, 0), (0, LANES - 2 * nh))).astype(BF16)
    gate_bias = jnp.concatenate([mlstm_b_i[0], mlstm_b_f[0]])
    brow = jnp.pad(gate_bias, (0, LANES - 2 * nh)).reshape(1, LANES)
    bcol = gate_bias.reshape(2 * nh, 1)
    gh = mlstm_g_h[0].reshape(1, nh * dv)
    wo1 = mlstm_w_out[0].astype(BF16)

    def tri_incl(n):
        idx = jnp.arange(n)
        return (idx[None, :] <= idx[:, None]).astype(BF16)

    def neg_suffix(n):
        idx = jnp.arange(n)
        return -(idx[None, :] > idx[:, None]).astype(BF16)

    def fox_layer_short(xr, b, t, cache):
        xp = _pad_rows(xr, b, t, PAD_T)
        q, k32, v32, kbf, vT, logf = fox_inproj(xp, g_mix2[0], wqkv, wf, bf, gq, gk, s_mat, st2_mat, PAD_T)
        ka, qa = cumsum_aug(logf, tri_incl(PAD_T), b, PAD_T)
        o = attn_small(q, qa, kbf, ka, vT, b, cache)
        return xp, q, k32, v32, kbf, vT, logf, ka, qa, o

    def mlstm_layer(x1, b, t, tp, c0, n0, m0):
        rows = b * tp
        tm = ROW_TILE if rows % ROW_TILE == 0 else MLSTM_CHUNK
        xp = _pad_rows(x1, b, t, tp) if tp != t else x1
        q, k, v, o, gates = mlstm_inproj(xp, g_mix2[1], wm, wg, tm)
        if tp != t:
            is_pad = (jnp.arange(rows) % tp >= t)[:, None]
            pad_val = jnp.where(jnp.arange(LANES) < nh, NEG, BIG).astype(F32)[None, :]
            gates = jnp.where(is_pad, pad_val, gates)
        gates_r = gates[:, :2 * nh].T
        m0b = jnp.broadcast_to(m0[..., None, None], (b, nh, 1, LANES))
        a, c_new, n_new, m_new = mlstm(q, k, v, o, gates, gates_r, brow, bcol, gh,
                                       c0, n0.reshape(b, nh, 1, dk), m0b, b)
        if tp != t:
            a = _unpad_rows(a, b, t, tp)
        return a, c_new, n_new.reshape(b, nh, dk), m_new[:, :, 0, 0]

    xm = meta_tokens.astype(F32)
    xm_p, _, km32, vm32, kmbf, vTm, logfm, kam_incl, qam, om = fox_layer_short(xm, 1, N_META, None)
    kam_rel, _ = cumsum_aug(logfm * (jnp.arange(PAD_T) < N_META)[:, None].astype(F32),
                            neg_suffix(PAD_T), 1, PAD_T)
    xm1 = post(om[:N_META], xm, wo0, g_ffn2[0], wu[0], wd[0], gfin, N_META, False)
    zc = jnp.zeros((1, nh, dv, dk), F32)
    zn = jnp.zeros((1, nh, dk), F32)
    zm = jnp.zeros((1, nh), F32)
    _, cm, nm, mm = mlstm_layer(xm1, 1, N_META, MLSTM_CHUNK, zc, zn, zm)

    rows = batch * seq
    xr = x_prompt.reshape(rows, d)
    q, k32, v32, kbf, vT, logf = fox_inproj(xr, g_mix2[0], wqkv, wf, bf, gq, gk, s_mat, st2_mat, ROW_TILE)
    ka, qa = cumsum_aug(logf, tri_incl(ROW_TILE), batch, ROW_TILE)
    o = attn_main(q, qa, kbf, ka, vT, kmbf, kam_rel, vTm, batch, seq)
    x1 = post(o, xr, wo0, g_ffn2[0], wu[0], wd[0], gfin, ROW_TILE, False)
    a, c_p, n_p, m_p = mlstm_layer(x1, batch, seq, seq,
                                   jnp.broadcast_to(cm, (batch, nh, dv, dk)),
                                   jnp.broadcast_to(nm, (batch, nh, dk)),
                                   jnp.broadcast_to(mm, (batch, nh)))
    y_prompt = post(a, x1, wo1, g_ffn2[1], wu[1], wd[1], gfin, ROW_TILE, True).reshape(batch, seq, d)

    def with_meta(meta_part, frame_part, width):
        mpart = jnp.broadcast_to(meta_part[None, :N_META, :width], (batch, N_META, width))
        return jnp.concatenate([mpart, frame_part[:, :width].reshape(batch, seq, width)], axis=1)

    fk_p = with_meta(km32, k32, d).reshape(1, batch, N_META + seq, nhf, dh)
    fv_p = with_meta(vm32, v32, d).reshape(1, batch, N_META + seq, nhf, dh)
    fl_p = with_meta(logfm, logf, nhf).reshape(1, batch, N_META + seq, nhf)

    srows = dbatch * dseq
    xs = x_sample.reshape(srows, d)
    logf_past = jnp.pad(cache_fox_logf[0].astype(F32).reshape(dbatch * past, nhf),
                        ((0, 0), (0, LANES - nhf)))
    ka_past, _ = cumsum_aug(logf_past, neg_suffix(past), dbatch, past)
    cache = (cache_fox_k[0].astype(F32).reshape(dbatch * past, d),
             cache_fox_v[0].astype(F32).reshape(dbatch * past, d), ka_past)
    _, _, ks32, vs32, _, _, logfs, _, _, os_ = fox_layer_short(xs, dbatch, dseq, cache)
    stm = ROW_TILE if srows % ROW_TILE == 0 else dseq
    xs1 = post(_unpad_rows(os_, dbatch, dseq, PAD_T), xs, wo0, g_ffn2[0], wu[0], wd[0], gfin, stm, False)
    tp = -(-dseq // MLSTM_CHUNK) * MLSTM_CHUNK
    a_s, c_s, n_s, m_s = mlstm_layer(xs1, dbatch, dseq, tp, state_mlstm_C[0].astype(F32),
                                     state_mlstm_n[0].astype(F32), state_mlstm_m[0].astype(F32))
    y_sample = post(a_s, xs1, wo1, g_ffn2[1], wu[1], wd[1], gfin, stm, True).reshape(dbatch, dseq, d)

    def short(a, width):
        return _unpad_rows(a[:, :width], dbatch, dseq, PAD_T)

    fk_s = short(ks32, d).reshape(1, dbatch, dseq, nhf, dh)
    fv_s = short(vs32, d).reshape(1, dbatch, dseq, nhf, dh)
    fl_s = short(logfs, nhf).reshape(1, dbatch, dseq, nhf)

    return (y_prompt, y_sample, fk_p, fv_p, fl_p, c_p[None], n_p[None], m_p[None],
            fk_s, fv_s, fl_s, c_s[None], n_s[None], m_s[None])
```
